```python
import jax
import jax.numpy as jnp
from jax import lax
import numpy as np

D_MODEL = 2048
BATCH = 4
SEQ = 2048
DEPTH = 1

GRID_W = 64
CTX_LEN = 256
N_HEADS_M = 8
DK_M = 128
DV_M = 256
D_QK = N_HEADS_M * DK_M
D_V = N_HEADS_M * DV_M
CHUNK = 64
QK_CONV_W = 3
D_CONV = D_MODEL
CONV_W = 31
D_FF = (8 * D_MODEL + 3 * 256 - 1) // (3 * 256) * 256
ALPHA = (2.0 * DEPTH) ** 0.25
BETA = (8.0 * DEPTH) ** -0.25
LN_EPS = 1e-5
N_GATES_M = 4 * N_HEADS_M
SPLITS = (D_QK, 2 * D_QK, 2 * D_QK + D_V, 2 * D_QK + 2 * D_V, 2 * D_QK + 2 * D_V + N_GATES_M, 2 * D_QK + 2 * D_V + N_GATES_M + 2 * D_CONV)
W_IN_COLS = SPLITS[-1] + 2 * D_MODEL

kernel_name = 'hybrid_mlstm_conformer_dit_block'


def layer_norm(x, g=None, b=None):
    xf = x.astype(jnp.float32)
    mu = xf.mean(-1, keepdims=True)
    var = jnp.square(xf - mu).mean(-1, keepdims=True)
    y = (xf - mu) * lax.rsqrt(var + LN_EPS)
    if g is not None:
        y = y * g.astype(jnp.float32)
    if b is not None:
        y = y + b.astype(jnp.float32)
    return y.astype(x.dtype)


def modulate(y, shift, scale):
    return y * (1.0 + scale) + shift


def dwconv(x, w, b):
    y = lax.conv_general_dilated(x, w[:, None, :].astype(x.dtype), (1,), 'SAME',
                                 dimension_numbers=('NWC', 'WIO', 'NWC'),
                                 feature_group_count=x.shape[-1])
    return y + b


def to_heads(a, d):
    B, T, _ = a.shape
    return a.reshape(B, T, -1, d).transpose(0, 2, 1, 3)


def flip_t(a):
    return jnp.flip(a, axis=2)


def zero_state(B):
    f32 = jnp.float32
    return (jnp.zeros((B, N_HEADS_M, DK_M, DV_M), f32), jnp.zeros((B, N_HEADS_M, DK_M), f32),
            jnp.zeros((B, N_HEADS_M), f32))


def mlstm_chunkwise(q, k, v, log_i, log_f, state):
    B, H, T, _ = q.shape
    f32 = jnp.float32
    tril = jnp.tril(jnp.ones((CHUNK, CHUNK), dtype=bool))

    def to_chunks(a):
        a = a.astype(f32).reshape(B, H, T // CHUNK, CHUNK, *a.shape[3:])
        return jnp.moveaxis(a, 2, 0)

    def step(carry, blk):
        C, n, m = carry
        qc, kc, vc, ic, fc = blk
        b = jnp.cumsum(fc, axis=-1)
        d_log = jnp.where(tril, b[..., :, None] - b[..., None, :] + ic[..., None, :], -jnp.inf)
        inter = b + m[..., None]
        m_t = jnp.maximum(inter, d_log.max(-1))
        w_intra = jnp.exp(d_log - m_t[..., None])
        w_inter = jnp.exp(inter - m_t)
        s = jnp.einsum('bhtd,bhsd->bhts', qc, kc) * w_intra
        num = w_inter[..., None] * jnp.einsum('bhtd,bhde->bhte', qc, C) + jnp.einsum('bhts,bhse->bhte', s, vc)
        den = w_inter * jnp.einsum('bhtd,bhd->bht', qc, n) + s.sum(-1)
        h = num / jnp.maximum(jnp.abs(den), jnp.exp(-m_t))[..., None]
        b_end = b[..., -1]
        w_log = b_end[..., None] - b + ic
        m_new = jnp.maximum(b_end + m, w_log.max(-1))
        decay = jnp.exp(b_end + m - m_new)
        wk = jnp.exp(w_log - m_new[..., None])[..., None] * kc
        C = decay[..., None, None] * C + jnp.einsum('bhsd,bhse->bhde', wk, vc)
        n = decay[..., None] * n + wk.sum(-2)
        return (C, n, m_new), h

    blocks = tuple(to_chunks(a) for a in (q, k, v, log_i, log_f))
    state, h = lax.scan(step, state, blocks)
    h = jnp.moveaxis(h, 0, 2).reshape(B, H, T, -1)
    return h.astype(v.dtype), state


def gate_logs(g):
    g = jnp.moveaxis(g.astype(jnp.float32), -1, 1)
    i_f, f_f, i_b, f_b = jnp.split(g, 4, axis=1)
    return i_f, jax.nn.log_sigmoid(f_f), i_b, jax.nn.log_sigmoid(f_b)


def mixer_inputs(u, w_in, b_if, w_qkc, b_qkc):
    p = u @ w_in
    q, k, v, o, g_if, conv_in, g_merge = jnp.split(p, SPLITS, axis=-1)
    qk = jax.nn.silu(dwconv(jnp.concatenate([q, k], axis=-1), w_qkc, b_qkc))
    q, k = jnp.split(qk, 2, axis=-1)
    q = to_heads(q, DK_M) * (DK_M ** -0.5)
    return q, to_heads(k, DK_M), to_heads(v, DV_M), o, g_if + b_if, conv_in, g_merge


def mixer_output(h, o, conv_in, g_merge, rows, mh_g, w_dw, b_dw, cln_g, cln_b, w_pm, w_pc, w_o):
    B, H, T, _ = h.shape
    hm = layer_norm(h).transpose(0, 2, 1, 3).reshape(B, T, D_V) * mh_g
    y_m = (hm * jax.nn.sigmoid(o)) @ w_pm
    a, gl = jnp.split(conv_in, 2, axis=-1)
    y = a * jax.nn.sigmoid(gl)
    C = y.shape[-1]
    if rows is not None:
        y = dwconv(y.reshape(B * rows, T // rows, C), w_dw, b_dw).reshape(B, T, C)
    else:
        y = dwconv(y, w_dw, b_dw)
    y_c = jax.nn.silu(layer_norm(y, cln_g, cln_b)) @ w_pc
    ga, gb = jnp.split(g_merge, 2, axis=-1)
    return (jax.nn.sigmoid(ga) * y_m + jax.nn.sigmoid(gb) * y_c) @ w_o


def swiglu(u, w_in, w_out):
    a, b = jnp.split(u @ w_in, 2, axis=-1)
    return (jax.nn.silu(a) * b) @ w_out


def setup_inputs(seed: int = 0) -> dict:
    key = jax.random.key(seed)
    ks = jax.random.split(key, 26)
    f32 = jnp.float32
    L = DEPTH

    def nrm(k, shape, scale):
        return jax.random.normal(k, shape, f32) * scale

    fan = D_MODEL ** -0.5
    lin = jnp.linspace(3.0, 6.0, N_HEADS_M, dtype=f32)
    zer = jnp.zeros((N_HEADS_M,), f32)
    if_base = jnp.concatenate([zer, lin, zer, lin])
    return {
        'x': nrm(ks[0], (BATCH, SEQ, D_MODEL), 1.0),
        'c': nrm(ks[1], (BATCH, D_MODEL), 1.0),
        'ctx': nrm(ks[2], (BATCH, CTX_LEN, D_MODEL), 1.0),
        'c_ctx': nrm(ks[3], (D_MODEL,), 1.0),
        'w_mod': nrm(ks[4], (L, D_MODEL, 6 * D_MODEL), 0.5 * fan),
        'b_mod': nrm(ks[5], (L, 6 * D_MODEL), 0.02),
        'w_in': nrm(ks[6], (L, D_MODEL, W_IN_COLS), fan),
        'b_if': if_base[None, :] + nrm(ks[7], (L, N_GATES_M), 0.1),
        'w_qk_conv': nrm(ks[8], (L, QK_CONV_W, 2 * D_QK), QK_CONV_W ** -0.5),
        'b_qk_conv': nrm(ks[9], (L, 2 * D_QK), 0.02),
        'mh_norm_g': 1.0 + nrm(ks[10], (L, D_V), 0.02),
        'w_dw': nrm(ks[11], (L, CONV_W, D_CONV), CONV_W ** -0.5),
        'b_dw': nrm(ks[12], (L, D_CONV), 0.02),
        'conv_ln_g': 1.0 + nrm(ks[13], (L, D_CONV), 0.02),
        'conv_ln_b': nrm(ks[14], (L, D_CONV), 0.02),
        'w_proj_m': nrm(ks[15], (L, D_V, D_MODEL), BETA * D_V ** -0.5),
        'w_proj_c': nrm(ks[16], (L, D_CONV, D_MODEL), BETA * D_CONV ** -0.5),
        'w_out': nrm(ks[17], (L, D_MODEL, D_MODEL), BETA * fan),
        'ln1_g': 1.0 + nrm(ks[18], (L, D_MODEL), 0.02),
        'ln1_b': nrm(ks[19], (L, D_MODEL), 0.02),
        'w_ffn_in': nrm(ks[20], (L, D_MODEL, 2 * D_FF), fan),
        'w_ffn_out': nrm(ks[21], (L, D_FF, D_MODEL), BETA * D_FF ** -0.5),
        'ln2_g': 1.0 + nrm(ks[22], (L, D_MODEL), 0.02),
        'ln2_b': nrm(ks[23], (L, D_MODEL), 0.02),
    }


def reference(x, c, ctx, c_ctx, w_mod, b_mod, w_in, b_if, w_qk_conv, b_qk_conv, mh_norm_g, w_dw, b_dw,
              conv_ln_g, conv_ln_b, w_proj_m, w_proj_c, w_out, ln1_g, ln1_b, w_ffn_in, w_ffn_out, ln2_g, ln2_b):
    B, T, _ = x.shape
    ROWS = T // GRID_W
    for l in range(DEPTH):
        last = l == DEPTH - 1
        sh1, sc1, g1, sh2, sc2, g2 = [m[:, None, :] for m in jnp.split(jax.nn.silu(c) @ w_mod[l] + b_mod[l], 6, axis=-1)]
        csh1, csc1, cg1, csh2, csc2, cg2 = jnp.split(jax.nn.silu(c_ctx) @ w_mod[l] + b_mod[l], 6, axis=-1)
        u = modulate(layer_norm(x), sh1, sc1)
        uc = modulate(layer_norm(ctx), csh1, csc1)
        qx, kx, vx, ox, gx, cvx, mx = mixer_inputs(u, w_in[l], b_if[l], w_qk_conv[l], b_qk_conv[l])
        qc, kc, vc, oc, gc, cvc, mc = mixer_inputs(uc, w_in[l], b_if[l], w_qk_conv[l], b_qk_conv[l])
        ix_f, fx_f, ix_b, fx_b = gate_logs(gx)
        ic_f, fc_f, ic_b, fc_b = gate_logs(gc)
        zero = zero_state(B)
        hc_f, st_f = mlstm_chunkwise(qc, kc, vc, ic_f, fc_f, zero)
        hc_b, st_b = mlstm_chunkwise(*[flip_t(a) for a in (qc, kc, vc, ic_b, fc_b)], zero)
        hx_f, _ = mlstm_chunkwise(qx, kx, vx, ix_f, fx_f, st_f)
        hx_b, _ = mlstm_chunkwise(*[flip_t(a) for a in (qx, kx, vx, ix_b, fx_b)], st_b)
        hx = hx_f + flip_t(hx_b)
        out_params = (mh_norm_g[l], w_dw[l], b_dw[l], conv_ln_g[l], conv_ln_b[l], w_proj_m[l], w_proj_c[l], w_out[l])
        yx = mixer_output(hx, ox, cvx, mx, ROWS, *out_params)
        x = layer_norm(ALPHA * x + g1 * yx, ln1_g[l], ln1_b[l])
        yf = swiglu(modulate(layer_norm(x), sh2, sc2), w_ffn_in[l], w_ffn_out[l])
        x = layer_norm(ALPHA * x + g2 * yf, ln2_g[l], ln2_b[l])
        if not last:
            hc = hc_f + flip_t(hc_b)
            yc = mixer_output(hc, oc, cvc, mc, None, *out_params)
            ctx = layer_norm(ALPHA * ctx + cg1 * yc, ln1_g[l], ln1_b[l])
            yfc = swiglu(modulate(layer_norm(ctx), csh2, csc2), w_ffn_in[l], w_ffn_out[l])
            ctx = layer_norm(ALPHA * ctx + cg2 * yfc, ln2_g[l], ln2_b[l])
    return x
```

```python
import functools

import jax
import jax.numpy as jnp
from jax import lax
from jax.experimental import pallas as pl
from jax.experimental.pallas import tpu as pltpu

D_MODEL = 2048
GRID_W = 64
N_HEADS = 8
DK = 128
DV = 256
D_QK = N_HEADS * DK
D_V = N_HEADS * DV
CONV_W = 31
CONV_HALF = CONV_W // 2
N_GATES = 4 * N_HEADS
D_FF = 5632
DEPTH = 1
ALPHA = (2.0 * DEPTH) ** 0.25
LN_EPS = 1e-5
GATE_COL0 = 2 * D_QK + 2 * D_V
N_MAIN = 2 * D_QK + 2 * D_V + 2 * D_MODEL + 2 * D_MODEL

MLSTM_CHUNK = 256
NEG_BIG = -1e30
LANES = 128
ROW_CHUNK = 256
V7X_VMEM_LIMIT = 56 * 1024 * 1024

F32 = jnp.float32
BF16 = jnp.bfloat16


def _cparams(sem):
    return pltpu.CompilerParams(dimension_semantics=sem, vmem_limit_bytes=V7X_VMEM_LIMIT)


def _ln_rows(x):
    mu = jnp.mean(x, axis=-1, keepdims=True)
    xc = x - mu
    var = jnp.mean(xc * xc, axis=-1, keepdims=True)
    return xc * lax.rsqrt(var + LN_EPS)


def _sigmoid(x):
    return 1.0 / (1.0 + jnp.exp(-x))


def _silu(x):
    return x * _sigmoid(x)


def _log_sigmoid(x):
    return jnp.minimum(x, 0.0) - jnp.log1p(jnp.exp(-jnp.abs(x)))


def _for_row_chunks(n_rows, body):
    chunk = min(ROW_CHUNK, n_rows)

    def step(c, carry):
        body(pl.ds(pl.multiple_of(c * chunk, chunk), chunk))
        return carry

    lax.fori_loop(0, n_rows // chunk, step, 0)


def _mod_kernel(c_ref, w_ref, b_ref, o_ref):
    s = _silu(c_ref[...])
    o_ref[...] = jnp.dot(s, w_ref[...], preferred_element_type=F32,
                         precision=lax.Precision.HIGHEST) + b_ref[...]


def _modulation(c_all, w_mod, b_mod, tn=1024):
    rows, d = c_all.shape
    n = w_mod.shape[1]
    return pl.pallas_call(
        _mod_kernel,
        grid=(n // tn,),
        in_specs=[pl.BlockSpec((rows, d), lambda j: (0, 0)),
                  pl.BlockSpec((d, tn), lambda j: (0, j)),
                  pl.BlockSpec((1, tn), lambda j: (0, j))],
        out_specs=pl.BlockSpec((rows, tn), lambda j: (0, j)),
        out_shape=jax.ShapeDtypeStruct((rows, n), F32),
        compiler_params=_cparams(("arbitrary",)),
        name="adaln_mod",
    )(c_all, w_mod, b_mod)


def _inproj_kernel(x_ref, sh_ref, sc_ref, w_ref, wg_ref, o_ref, g_ref, u_ref):
    @pl.when(pl.program_id(1) == 0)
    def _():
        def ln_mod(rs):
            u = _ln_rows(x_ref[rs, :]) * (1.0 + sc_ref[...]) + sh_ref[...]
            ub = u.astype(BF16)
            u_ref[rs, :] = ub
            g_ref[rs, :] = jnp.dot(ub, wg_ref[...], preferred_element_type=F32)

        _for_row_chunks(x_ref.shape[0], ln_mod)

    o_ref[...] = jnp.dot(u_ref[...], w_ref[...], preferred_element_type=F32).astype(BF16)


def _in_projection(x2d, shift, scale, w_main, w_gate, n_cols, rows_per_batch, tm, tn):
    m, d = x2d.shape
    tiles_per_batch = rows_per_batch // tm
    return pl.pallas_call(
        _inproj_kernel,
        grid=(m // tm, n_cols // tn),
        in_specs=[pl.BlockSpec((tm, d), lambda i, j: (i, 0)),
                  pl.BlockSpec((None, 1, d), lambda i, j: (i // tiles_per_batch, 0, 0)),
                  pl.BlockSpec((None, 1, d), lambda i, j: (i // tiles_per_batch, 0, 0)),
                  pl.BlockSpec((d, tn), lambda i, j: (0, j)),
                  pl.BlockSpec((d, LANES), lambda i, j: (0, 0))],
        out_specs=[pl.BlockSpec((tm, tn), lambda i, j: (i, j)),
                   pl.BlockSpec((tm, LANES), lambda i, j: (i, 0))],
        out_shape=[jax.ShapeDtypeStruct((m, n_cols), BF16),
                   jax.ShapeDtypeStruct((m, LANES), F32)],
        scratch_shapes=[pltpu.VMEM((tm, d), BF16)],
        compiler_params=_cparams(("parallel", "arbitrary")),
        name="in_projection",
    )(x2d, shift, scale, w_main, w_gate)


def _qk_conv_silu(x, w_ref, b_ref, scale):
    t = x.shape[0]
    row = lax.broadcasted_iota(jnp.int32, x.shape, 0)
    prev = jnp.where(row == 0, 0.0, pltpu.roll(x, 1, 0))
    nxt = jnp.where(row == t - 1, 0.0, pltpu.roll(x, t - 1, 0))
    y = prev * w_ref[0:1, :] + x * w_ref[1:2, :] + nxt * w_ref[2:3, :] + b_ref[...]
    y = _silu(y)
    if scale != 1.0:
        y = y * scale
    return y.astype(BF16)


def _mlstm_chunk(q, k, v, i_row, f_row, state, reverse, need_h):
    c_st, n_st, m_st = state
    length = q.shape[0]
    t_id = lax.broadcasted_iota(jnp.int32, (length, length), 0)
    s_id = lax.broadcasted_iota(jnp.int32, (length, length), 1)
    causal = (s_id >= t_id) if reverse else (s_id <= t_id)
    eye = s_id == t_id
    b_col = jnp.sum(jnp.where(causal, f_row, 0.0), axis=1, keepdims=True)
    b_row = jnp.sum(jnp.where(eye, b_col, 0.0), axis=0, keepdims=True)
    i_col = jnp.sum(jnp.where(eye, i_row, 0.0), axis=1, keepdims=True)
    h = None
    if need_h:
        d_log = jnp.where(causal, b_col - b_row + i_row, NEG_BIG)
        inter = b_col + m_st
        m_t = jnp.maximum(inter, jnp.max(d_log, axis=1, keepdims=True))
        w_intra = jnp.exp(d_log - m_t)
        w_inter = jnp.exp(inter - m_t)
        s = lax.dot_general(q, k, (((1,), (1,)), ((), ())), preferred_element_type=F32) * w_intra
        q_c = jnp.dot(q, c_st.astype(BF16), preferred_element_type=F32)
        s_v = jnp.dot(s.astype(BF16), v, preferred_element_type=F32)
        num = w_inter * q_c + s_v
        q_n = jnp.sum(q.astype(F32) * n_st, axis=1, keepdims=True)
        den = w_inter * q_n + jnp.sum(s, axis=1, keepdims=True)
        h = num * (1.0 / jnp.maximum(jnp.abs(den), jnp.exp(-m_t)))
    b_end = b_col[0:1, :] if reverse else b_col[length - 1:length, :]
    w_log = b_end - b_col + i_col
    m_new = jnp.maximum(b_end + m_st, jnp.max(w_log, axis=0, keepdims=True))
    decay = jnp.exp(b_end + m_st - m_new)
    wk = jnp.exp(w_log - m_new) * k.astype(F32)
    c_new = decay * c_st + lax.dot_general(wk.astype(BF16), v, (((0,), (0,)), ((), ())),
                                           preferred_element_type=F32)
    n_new = decay * n_st + jnp.sum(wk, axis=0, keepdims=True)
    return h, (c_new, n_new, m_new)


def _mlstm_kernel(qx_ref, kx_ref, vx_ref, ox_ref, qc_ref, kc_ref, vc_ref, gx_ref, gc_ref, bif_ref,
                  wq_ref, wk_ref, bq_ref, bk_ref, mhg_ref, out_ref, qs_ref, ks_ref, hf_ref, hb_ref):
    t_len = qx_ref.shape[0]
    chunk = MLSTM_CHUNK
    n_chunks = t_len // chunk
    qs_ref[...] = _qk_conv_silu(qx_ref[...].astype(F32), wq_ref, bq_ref, DK ** -0.5)
    ks_ref[...] = _qk_conv_silu(kx_ref[...].astype(F32), wk_ref, bk_ref, 1.0)
    q_ctx = _qk_conv_silu(qc_ref[...].astype(F32), wq_ref, bq_ref, DK ** -0.5)
    k_ctx = _qk_conv_silu(kc_ref[...].astype(F32), wk_ref, bk_ref, 1.0)
    v_ctx = vc_ref[...]

    def gates(g_ref, idx):
        return g_ref[idx] + bif_ref[idx]

    zero_state = (jnp.zeros((DK, DV), F32), jnp.zeros((1, DK), F32), jnp.zeros((1, 1), F32))
    _, st_f = _mlstm_chunk(q_ctx, k_ctx, v_ctx, gates(gc_ref, 0), _log_sigmoid(gates(gc_ref, 1)),
                           zero_state, reverse=False, need_h=False)
    _, st_b = _mlstm_chunk(q_ctx, k_ctx, v_ctx, gates(gc_ref, 2), _log_sigmoid(gates(gc_ref, 3)),
                           zero_state, reverse=True, need_h=False)

    def rows(j):
        return slice(j * chunk, (j + 1) * chunk)

    for j in range(n_chunks):
        rf = rows(j)
        h, st_f = _mlstm_chunk(qs_ref[rf, :], ks_ref[rf, :], vx_ref[rf, :],
                               gx_ref[0, :, rf] + bif_ref[0],
                               _log_sigmoid(gx_ref[1, :, rf] + bif_ref[1]),
                               st_f, reverse=False, need_h=True)
        hf_ref[rf, :] = h
        rb = rows(n_chunks - 1 - j)
        h, st_b = _mlstm_chunk(qs_ref[rb, :], ks_ref[rb, :], vx_ref[rb, :],
                               gx_ref[2, :, rb] + bif_ref[2],
                               _log_sigmoid(gx_ref[3, :, rb] + bif_ref[3]),
                               st_b, reverse=True, need_h=True)
        hb_ref[rb, :] = h

    for j in range(n_chunks):
        r = rows(j)
        hm = _ln_rows(hf_ref[r, :] + hb_ref[r, :]) * mhg_ref[...]
        out_ref[r, :] = (hm * _sigmoid(ox_ref[r, :].astype(F32))).astype(BF16)


def _mlstm(px, pc, gx_rows, gc_rows, bif, w_qkc, b_qkc, mh_g):
    b, t, _ = px.shape
    tc = pc.shape[1]
    kq = D_QK // DK
    kv = 2 * D_QK // DV
    ko = (2 * D_QK + D_V) // DV
    return pl.pallas_call(
        _mlstm_kernel,
        grid=(b, N_HEADS),
        in_specs=[pl.BlockSpec((None, t, DK), lambda i, h: (i, 0, h)),
                  pl.BlockSpec((None, t, DK), lambda i, h: (i, 0, kq + h)),
                  pl.BlockSpec((None, t, DV), lambda i, h: (i, 0, kv + h)),
                  pl.BlockSpec((None, t, DV), lambda i, h: (i, 0, ko + h)),
                  pl.BlockSpec((None, tc, DK), lambda i, h: (i, 0, h)),
                  pl.BlockSpec((None, tc, DK), lambda i, h: (i, 0, kq + h)),
                  pl.BlockSpec((None, tc, DV), lambda i, h: (i, 0, kv + h)),
                  pl.BlockSpec((None, 4, None, 1, t), lambda i, h: (i, 0, h, 0, 0)),
                  pl.BlockSpec((None, 4, None, 1, tc), lambda i, h: (i, 0, h, 0, 0)),
                  pl.BlockSpec((4, None, 1, 1), lambda i, h: (0, h, 0, 0)),
                  pl.BlockSpec((3, DK), lambda i, h: (0, h)),
                  pl.BlockSpec((3, DK), lambda i, h: (0, kq + h)),
                  pl.BlockSpec((1, DK), lambda i, h: (0, h)),
                  pl.BlockSpec((1, DK), lambda i, h: (0, kq + h)),
                  pl.BlockSpec((1, DV), lambda i, h: (0, h))],
        out_specs=pl.BlockSpec((None, t, DV), lambda i, h: (i, 0, h)),
        out_shape=jax.ShapeDtypeStruct((b, t, D_V), BF16),
        scratch_shapes=[pltpu.VMEM((t, DK), BF16), pltpu.VMEM((t, DK), BF16),
                        pltpu.VMEM((t, DV), F32), pltpu.VMEM((t, DV), F32)],
        compiler_params=_cparams(("parallel", "parallel")),
        name="mlstm",
    )(px, px, px, px, pc, pc, pc, gx_rows, gc_rows, bif, w_qkc, w_qkc, b_qkc, b_qkc, mh_g)


CONV_PAD_LO = 16
CONV_SEG_ROWS = CONV_PAD_LO + GRID_W + 16


def _glu_conv_kernel(a_ref, gl_ref, w_ref, b_ref, o_ref, ypad_ref):
    tm, cb = a_ref.shape
    n_seg = tm // GRID_W
    zeros_pad = jnp.zeros((16, cb), F32)

    def fill(seg, carry):
        r0 = pl.multiple_of(seg * GRID_W, GRID_W)
        a = a_ref[pl.ds(r0, GRID_W), :].astype(F32)
        gl = gl_ref[pl.ds(r0, GRID_W), :].astype(F32)
        ypad_ref[seg, 0:CONV_PAD_LO, :] = zeros_pad
        ypad_ref[seg, CONV_PAD_LO:CONV_PAD_LO + GRID_W, :] = a * _sigmoid(gl)
        ypad_ref[seg, CONV_PAD_LO + GRID_W:CONV_SEG_ROWS, :] = zeros_pad
        return carry

    lax.fori_loop(0, n_seg, fill, 0)

    def conv(seg, carry):
        r0 = pl.multiple_of(seg * GRID_W, GRID_W)
        for lb in range(cb // LANES):
            ls = slice(lb * LANES, (lb + 1) * LANES)
            acc = jnp.zeros((GRID_W, LANES), F32) + b_ref[:, ls]
            for k in range(CONV_W):
                off = CONV_PAD_LO - CONV_HALF + k
                acc = acc + ypad_ref[seg, off:off + GRID_W, ls] * w_ref[k:k + 1, ls]
            o_ref[pl.ds(r0, GRID_W), ls] = acc.astype(BF16)
        return carry

    lax.fori_loop(0, n_seg, conv, 0)


def _glu_conv(px2d, w_dw, b_dw, tm=1024, cb=512):
    m = px2d.shape[0]
    a0 = (2 * D_QK + 2 * D_V) // cb
    g0 = a0 + D_MODEL // cb
    return pl.pallas_call(
        _glu_conv_kernel,
        grid=(m // tm, D_MODEL // cb),
        in_specs=[pl.BlockSpec((tm, cb), lambda i, j: (i, a0 + j)),
                  pl.BlockSpec((tm, cb), lambda i, j: (i, g0 + j)),
                  pl.BlockSpec((CONV_W, cb), lambda i, j: (0, j)),
                  pl.BlockSpec((1, cb), lambda i, j: (0, j))],
        out_specs=pl.BlockSpec((tm, cb), lambda i, j: (i, j)),
        out_shape=jax.ShapeDtypeStruct((m, D_MODEL), BF16),
        scratch_shapes=[pltpu.VMEM((tm // GRID_W, CONV_SEG_ROWS, cb), F32)],
        compiler_params=_cparams(("parallel", "parallel")),
        name="glu_dwconv",
    )(px2d, px2d, w_dw, b_dw)


def _merge_kernel(hm_ref, yc_ref, ga_ref, gb_ref, clg_ref, clb_ref, wpm_ref, wpc_ref, o_ref, ycn_ref):
    @pl.when(pl.program_id(1) == 0)
    def _():
        def ln_silu(rs):
            y = _ln_rows(yc_ref[rs, :].astype(F32)) * clg_ref[...] + clb_ref[...]
            ycn_ref[rs, :] = _silu(y).astype(BF16)

        _for_row_chunks(yc_ref.shape[0], ln_silu)

    y_m = jnp.dot(hm_ref[...], wpm_ref[...], preferred_element_type=F32)
    y_c = jnp.dot(ycn_ref[...], wpc_ref[...], preferred_element_type=F32)
    merged = _sigmoid(ga_ref[...].astype(F32)) * y_m + _sigmoid(gb_ref[...].astype(F32)) * y_c
    o_ref[...] = merged.astype(BF16)


def _merge(hm2d, yconv, px2d, cln_g, cln_b, w_pm, w_pc, tm=1024, tn=512):
    m = hm2d.shape[0]
    ga0 = (N_MAIN - 2 * D_MODEL) // tn
    gb0 = ga0 + D_MODEL // tn
    return pl.pallas_call(
        _merge_kernel,
        grid=(m // tm, D_MODEL // tn),
        in_specs=[pl.BlockSpec((tm, D_V), lambda i, j: (i, 0)),
                  pl.BlockSpec((tm, D_MODEL), lambda i, j: (i, 0)),
                  pl.BlockSpec((tm, tn), lambda i, j: (i, ga0 + j)),
                  pl.BlockSpec((tm, tn), lambda i, j: (i, gb0 + j)),
                  pl.BlockSpec((1, D_MODEL), lambda i, j: (0, 0)),
                  pl.BlockSpec((1, D_MODEL), lambda i, j: (0, 0)),
                  pl.BlockSpec((D_V, tn), lambda i, j: (0, j)),
                  pl.BlockSpec((D_MODEL, tn), lambda i, j: (0, j))],
        out_specs=pl.BlockSpec((tm, tn), lambda i, j: (i, j)),
        out_shape=jax.ShapeDtypeStruct((m, D_MODEL), BF16),
        scratch_shapes=[pltpu.VMEM((tm, D_MODEL), BF16)],
        compiler_params=_cparams(("parallel", "arbitrary")),
        name="branch_merge",
    )(hm2d, yconv, px2d, px2d, cln_g, cln_b, w_pm, w_pc)


def _outproj_kernel(mg_ref, x_ref, g1_ref, sh2_ref, sc2_ref, lg_ref, lb_ref, wo_ref, x1_ref, u2_ref, y_ref):
    y_ref[...] = jnp.dot(mg_ref[...], wo_ref[...], preferred_element_type=F32)

    def post_ln(rs):
        x1 = _ln_rows(ALPHA * x_ref[rs, :] + g1_ref[...] * y_ref[rs, :]) * lg_ref[...] + lb_ref[...]
        x1_ref[rs, :] = x1
        u2_ref[rs, :] = (_ln_rows(x1) * (1.0 + sc2_ref[...]) + sh2_ref[...]).astype(BF16)

    _for_row_chunks(x_ref.shape[0], post_ln)


def _out_projection(merged, x2d, g1, sh2, sc2, ln_g, ln_b, w_o, rows_per_batch, tm=256):
    m, d = x2d.shape
    tpb = rows_per_batch // tm
    per_batch = pl.BlockSpec((None, 1, d), lambda i: (i // tpb, 0, 0))
    vec = pl.BlockSpec((1, d), lambda i: (0, 0))
    tile = pl.BlockSpec((tm, d), lambda i: (i, 0))
    return pl.pallas_call(
        _outproj_kernel,
        grid=(m // tm,),
        in_specs=[tile, tile, per_batch, per_batch, per_batch, vec, vec,
                  pl.BlockSpec((d, d), lambda i: (0, 0))],
        out_specs=[tile, tile],
        out_shape=[jax.ShapeDtypeStruct((m, d), F32), jax.ShapeDtypeStruct((m, d), BF16)],
        scratch_shapes=[pltpu.VMEM((tm, d), F32)],
        compiler_params=_cparams(("parallel",)),
        name="out_projection",
    )(merged, x2d, g1, sh2, sc2, ln_g, ln_b, w_o)


def _ffn_kernel(u_ref, x1_ref, g2_ref, lg_ref, lb_ref, wa_ref, wb_ref, wo_ref, o_ref, acc_ref):
    f = pl.program_id(1)

    @pl.when(f == 0)
    def _():
        acc_ref[...] = jnp.zeros_like(acc_ref)

    u = u_ref[...]
    a = jnp.dot(u, wa_ref[...], preferred_element_type=F32)
    b = jnp.dot(u, wb_ref[...], preferred_element_type=F32)
    hid = (_silu(a) * b).astype(BF16)
    acc_ref[...] += jnp.dot(hid, wo_ref[...], preferred_element_type=F32)

    @pl.when(f == pl.num_programs(1) - 1)
    def _():
        def post_ln(rs):
            y = ALPHA * x1_ref[rs, :] + g2_ref[...] * acc_ref[rs, :]
            o_ref[rs, :] = _ln_rows(y) * lg_ref[...] + lb_ref[...]

        _for_row_chunks(x1_ref.shape[0], post_ln)


def _ffn(u2, x1, g2, ln_g, ln_b, w_ffn_in, w_ffn_out, rows_per_batch, tm=512, tf=512):
    m, d = x1.shape
    tpb = rows_per_batch // tm
    nf = D_FF // tf
    tile = pl.BlockSpec((tm, d), lambda i, f: (i, 0))
    vec = pl.BlockSpec((1, d), lambda i, f: (0, 0))
    return pl.pallas_call(
        _ffn_kernel,
        grid=(m // tm, nf),
        in_specs=[tile, tile,
                  pl.BlockSpec((None, 1, d), lambda i, f: (i // tpb, 0, 0)),
                  vec, vec,
                  pl.BlockSpec((d, tf), lambda i, f: (0, f)),
                  pl.BlockSpec((d, tf), lambda i, f: (0, nf + f)),
                  pl.BlockSpec((tf, d), lambda i, f: (f, 0))],
        out_specs=tile,
        out_shape=jax.ShapeDtypeStruct((m, d), F32),
        scratch_shapes=[pltpu.VMEM((tm, d), F32)],
        compiler_params=_cparams(("parallel", "arbitrary")),
        name="swiglu_ffn",
    )(u2, x1, g2, ln_g, ln_b, w_ffn_in, w_ffn_in, w_ffn_out)


def _gate_rows(g, batch, t):
    g = g[:, :N_GATES].reshape(batch, t, 4, N_HEADS)
    return jnp.transpose(g, (0, 2, 3, 1))[:, :, :, None, :]


def kernel(x, c, ctx, c_ctx, w_mod, b_mod, w_in, b_if, w_qk_conv, b_qk_conv, mh_norm_g, w_dw, b_dw,
           conv_ln_g, conv_ln_b, w_proj_m, w_proj_c, w_out, ln1_g, ln1_b, w_ffn_in, w_ffn_out, ln2_g, ln2_b):
    assert w_in.shape[0] == DEPTH == 1
    batch, t, d = x.shape
    tc = ctx.shape[1]
    row = lambda v: v[0].reshape(1, -1)

    w_in0 = w_in[0]
    w_main = jnp.concatenate([w_in0[:, :GATE_COL0], w_in0[:, GATE_COL0 + N_GATES:]], axis=1).astype(BF16)
    w_gate = jnp.pad(w_in0[:, GATE_COL0:GATE_COL0 + N_GATES], ((0, 0), (0, LANES - N_GATES))).astype(BF16)
    w_pm, w_pc, w_o = w_proj_m[0].astype(BF16), w_proj_c[0].astype(BF16), w_out[0].astype(BF16)
    w_f_in, w_f_out = w_ffn_in[0].astype(BF16), w_ffn_out[0].astype(BF16)

    c_rows = 8
    c_all = jnp.concatenate([c, c_ctx[None, :], jnp.zeros((c_rows - batch - 1, d), F32)], axis=0)
    mods = _modulation(c_all, w_mod[0], b_mod[0].reshape(1, -1))
    sh1, sc1, g1, sh2, sc2, g2 = [mods[:batch, k * d:(k + 1) * d].reshape(batch, 1, d) for k in range(6)]
    csh1 = mods[batch:batch + 1, 0:d].reshape(1, 1, d)
    csc1 = mods[batch:batch + 1, d:2 * d].reshape(1, 1, d)

    x2d = x.reshape(batch * t, d)
    px, gx = _in_projection(x2d, sh1, sc1, w_main, w_gate, N_MAIN, t, tm=1024, tn=1024)
    pc, gc = _in_projection(ctx.reshape(batch * tc, d), csh1, csc1, w_main, w_gate,
                            2 * D_QK + D_V, batch * tc, tm=batch * tc, tn=1024)

    bif = b_if[0].reshape(4, N_HEADS, 1, 1)
    hm = _mlstm(px.reshape(batch, t, N_MAIN), pc.reshape(batch, tc, -1),
                _gate_rows(gx, batch, t), _gate_rows(gc, batch, tc), bif,
                w_qk_conv[0], row(b_qk_conv), row(mh_norm_g))

    yconv = _glu_conv(px, w_dw[0], row(b_dw))

    merged = _merge(hm.reshape(batch * t, D_V), yconv, px, row(conv_ln_g), row(conv_ln_b), w_pm, w_pc)
    x1, u2 = _out_projection(merged, x2d, g1, sh2, sc2, row(ln1_g), row(ln1_b), w_o, t)
    out = _ffn(u2, x1, g2, row(ln2_g), row(ln2_b), w_f_in, w_f_out, t)
    return out.reshape(batch, t, d)
```

```python
import functools

import jax
import jax.numpy as jnp
from jax import lax
from jax.experimental import pallas as pl
from jax.experimental.pallas import tpu as pltpu

D_MODEL = 2048
GRID_W = 64
N_HEADS = 8
DK = 128
DV = 256
D_QK = N_HEADS * DK
D_V = N_HEADS * DV
CONV_W = 31
CONV_HALF = CONV_W // 2
N_GATES = 4 * N_HEADS
D_FF = 5632
DEPTH = 1
ALPHA = (2.0 * DEPTH) ** 0.25
LN_EPS = 1e-5
GATE_COL0 = 2 * D_QK + 2 * D_V
N_MAIN = 2 * D_QK + 2 * D_V + 2 * D_MODEL + 2 * D_MODEL

MLSTM_CHUNK = 256
NEG_BIG = -1e30
LANES = 128
SUBLANES = 8
ROW_CHUNK = 256
V7X_VMEM_LIMIT = 56 * 1024 * 1024

F32 = jnp.float32
BF16 = jnp.bfloat16


def _cparams(sem):
    return pltpu.CompilerParams(dimension_semantics=sem, vmem_limit_bytes=V7X_VMEM_LIMIT)


def _ln_rows(x):
    mu = jnp.mean(x, axis=-1, keepdims=True)
    xc = x - mu
    var = jnp.mean(xc * xc, axis=-1, keepdims=True)
    return xc * lax.rsqrt(var + LN_EPS)


def _sigmoid(x):
    return 1.0 / (1.0 + jnp.exp(-x))


def _silu(x):
    return x * _sigmoid(x)


def _log_sigmoid(x):
    return jnp.minimum(x, 0.0) - jnp.log1p(jnp.exp(-jnp.abs(x)))


def _for_row_chunks(n_rows, body):
    chunk = min(ROW_CHUNK, n_rows)

    def step(c, carry):
        body(pl.ds(pl.multiple_of(c * chunk, chunk), chunk))
        return carry

    lax.fori_loop(0, n_rows // chunk, step, 0)


def _mod_kernel(c_ref, w_ref, b_ref, o_ref):
    s = _silu(c_ref[...])
    o_ref[...] = jnp.dot(s, w_ref[...], preferred_element_type=F32,
                         precision=lax.Precision.HIGHEST) + b_ref[...]


def _modulation(c_all, w_mod, b_mod, tn=1024):
    rows, d = c_all.shape
    n = w_mod.shape[1]
    return pl.pallas_call(
        _mod_kernel,
        grid=(n // tn,),
        in_specs=[pl.BlockSpec((rows, d), lambda j: (0, 0)),
                  pl.BlockSpec((d, tn), lambda j: (0, j)),
                  pl.BlockSpec((1, tn), lambda j: (0, j))],
        out_specs=pl.BlockSpec((rows, tn), lambda j: (0, j)),
        out_shape=jax.ShapeDtypeStruct((rows, n), F32),
        compiler_params=_cparams(("arbitrary",)),
        name="adaln_mod",
    )(c_all, w_mod, b_mod)


def _inproj_kernel(x_ref, sh_ref, sc_ref, wa_ref, wb_ref, wg_ref, o_ref, g_ref, u_ref, *, n_a):
    j = pl.program_id(1)

    @pl.when(j == 0)
    def _():
        def ln_mod(rs):
            u = _ln_rows(x_ref[rs, :]) * (1.0 + sc_ref[...]) + sh_ref[...]
            ub = u.astype(BF16)
            u_ref[rs, :] = ub
            g_ref[rs, :] = jnp.dot(ub, wg_ref[...], preferred_element_type=F32)

        _for_row_chunks(x_ref.shape[0], ln_mod)

    @pl.when(j < n_a)
    def _():
        o_ref[...] = jnp.dot(u_ref[...], wa_ref[...], preferred_element_type=F32).astype(BF16)

    @pl.when(j >= n_a)
    def _():
        o_ref[...] = jnp.dot(u_ref[...], wb_ref[...], preferred_element_type=F32).astype(BF16)


def _in_projection(x2d, shift, scale, w_a, w_b, w_gate, n_cols, rows_per_batch, tm, tn):
    m, d = x2d.shape
    tiles_per_batch = rows_per_batch // tm
    n_a = w_a.shape[1] // tn
    return pl.pallas_call(
        functools.partial(_inproj_kernel, n_a=n_a),
        grid=(m // tm, n_cols // tn),
        in_specs=[pl.BlockSpec((tm, d), lambda i, j: (i, 0)),
                  pl.BlockSpec((None, 1, d), lambda i, j: (i // tiles_per_batch, 0, 0)),
                  pl.BlockSpec((None, 1, d), lambda i, j: (i // tiles_per_batch, 0, 0)),
                  pl.BlockSpec((d, tn), lambda i, j: (0, jnp.minimum(j, n_a - 1))),
                  pl.BlockSpec((d, tn), lambda i, j: (0, jnp.maximum(j - n_a, 0))),
                  pl.BlockSpec((d, LANES), lambda i, j: (0, 0))],
        out_specs=[pl.BlockSpec((tm, tn), lambda i, j: (i, j)),
                   pl.BlockSpec((tm, LANES), lambda i, j: (i, 0))],
        out_shape=[jax.ShapeDtypeStruct((m, n_cols), BF16),
                   jax.ShapeDtypeStruct((m, LANES), F32)],
        scratch_shapes=[pltpu.VMEM((tm, d), BF16)],
        compiler_params=_cparams(("parallel", "arbitrary")),
        name="in_projection",
    )(x2d, shift, scale, w_a, w_b, w_gate)


def _qk_conv_silu(x, w_ref, b_ref, scale):
    t = x.shape[0]
    row = lax.broadcasted_iota(jnp.int32, x.shape, 0)
    prev = jnp.where(row == 0, 0.0, pltpu.roll(x, 1, 0))
    nxt = jnp.where(row == t - 1, 0.0, pltpu.roll(x, t - 1, 0))
    y = prev * w_ref[0:1, :] + x * w_ref[1:2, :] + nxt * w_ref[2:3, :] + b_ref[...]
    y = _silu(y)
    if scale != 1.0:
        y = y * scale
    return y.astype(BF16)


def _mlstm_chunk(q, k, v, i_row, f_row, state, reverse, need_h):
    c_st, n_st, m_st = state
    length = q.shape[0]
    t_id = lax.broadcasted_iota(jnp.int32, (length, length), 0)
    s_id = lax.broadcasted_iota(jnp.int32, (length, length), 1)
    causal = (s_id >= t_id) if reverse else (s_id <= t_id)
    eye = s_id == t_id
    b_col = jnp.sum(jnp.where(causal, f_row, 0.0), axis=1, keepdims=True)
    b_row = jnp.sum(jnp.where(eye, b_col, 0.0), axis=0, keepdims=True)
    i_col = jnp.sum(jnp.where(eye, i_row, 0.0), axis=1, keepdims=True)
    h = None
    if need_h:
        d_log = jnp.where(causal, b_col - b_row + i_row, NEG_BIG)
        inter = b_col + m_st
        m_t = jnp.maximum(inter, jnp.max(d_log, axis=1, keepdims=True))
        w_intra = jnp.exp(d_log - m_t)
        w_inter = jnp.exp(inter - m_t)
        s = lax.dot_general(q, k, (((1,), (1,)), ((), ())), preferred_element_type=F32) * w_intra
        q_c = jnp.dot(q, c_st.astype(BF16), preferred_element_type=F32)
        s_v = jnp.dot(s.astype(BF16), v, preferred_element_type=F32)
        num = w_inter * q_c + s_v
        q_n = jnp.sum(q.astype(F32) * n_st, axis=1, keepdims=True)
        den = w_inter * q_n + jnp.sum(s, axis=1, keepdims=True)
        h = num * (1.0 / jnp.maximum(jnp.abs(den), jnp.exp(-m_t)))
    b_end = b_col[0:1, :] if reverse else b_col[length - 1:length, :]
    w_log = b_end - b_col + i_col
    m_new = jnp.maximum(b_end + m_st, jnp.max(w_log, axis=0, keepdims=True))
    decay = jnp.exp(b_end + m_st - m_new)
    wk = jnp.exp(w_log - m_new) * k.astype(F32)
    c_new = decay * c_st + lax.dot_general(wk.astype(BF16), v, (((0,), (0,)), ((), ())),
                                           preferred_element_type=F32)
    n_new = decay * n_st + jnp.sum(wk, axis=0, keepdims=True)
    return h, (c_new, n_new, m_new)


def _mlstm_kernel(qx_ref, kx_ref, vx_ref, ox_ref, qc_ref, kc_ref, vc_ref, gx_ref, gc_ref, bif_ref,
                  wq_ref, wk_ref, bq_ref, bk_ref, mhg_ref, out_ref, qs_ref, ks_ref, hf_ref, hb_ref):
    t_len = qx_ref.shape[0]
    chunk = MLSTM_CHUNK
    n_chunks = t_len // chunk
    qs_ref[...] = _qk_conv_silu(qx_ref[...].astype(F32), wq_ref, bq_ref, DK ** -0.5)
    ks_ref[...] = _qk_conv_silu(kx_ref[...].astype(F32), wk_ref, bk_ref, 1.0)
    q_ctx = _qk_conv_silu(qc_ref[...].astype(F32), wq_ref, bq_ref, DK ** -0.5)
    k_ctx = _qk_conv_silu(kc_ref[...].astype(F32), wk_ref, bk_ref, 1.0)
    v_ctx = vc_ref[...]

    def gates(g_ref, idx):
        return g_ref[idx] + bif_ref[idx]

    zero_state = (jnp.zeros((DK, DV), F32), jnp.zeros((1, DK), F32), jnp.zeros((1, 1), F32))
    _, st_f = _mlstm_chunk(q_ctx, k_ctx, v_ctx, gates(gc_ref, 0), _log_sigmoid(gates(gc_ref, 1)),
                           zero_state, reverse=False, need_h=False)
    _, st_b = _mlstm_chunk(q_ctx, k_ctx, v_ctx, gates(gc_ref, 2), _log_sigmoid(gates(gc_ref, 3)),
                           zero_state, reverse=True, need_h=False)

    def rows(j):
        return slice(j * chunk, (j + 1) * chunk)

    for j in range(n_chunks):
        rf = rows(j)
        h, st_f = _mlstm_chunk(qs_ref[rf, :], ks_ref[rf, :], vx_ref[rf, :],
                               gx_ref[0, :, rf] + bif_ref[0],
                               _log_sigmoid(gx_ref[1, :, rf] + bif_ref[1]),
                               st_f, reverse=False, need_h=True)
        hf_ref[rf, :] = h
        rb = rows(n_chunks - 1 - j)
        h, st_b = _mlstm_chunk(qs_ref[rb, :], ks_ref[rb, :], vx_ref[rb, :],
                               gx_ref[2, :, rb] + bif_ref[2],
                               _log_sigmoid(gx_ref[3, :, rb] + bif_ref[3]),
                               st_b, reverse=True, need_h=True)
        hb_ref[rb, :] = h

    for j in range(n_chunks):
        r = rows(j)
        hm = _ln_rows(hf_ref[r, :] + hb_ref[r, :]) * mhg_ref[...]
        out_ref[r, :] = (hm * _sigmoid(ox_ref[r, :].astype(F32))).astype(BF16)


def _mlstm(px, pc, gx_rows, gc_rows, bif, w_qkc, b_qkc, mh_g):
    b, t, _ = px.shape
    tc = pc.shape[1]
    kq = D_QK // DK
    kv = 2 * D_QK // DV
    ko = (2 * D_QK + D_V) // DV
    return pl.pallas_call(
        _mlstm_kernel,
        grid=(b, N_HEADS),
        in_specs=[pl.BlockSpec((None, t, DK), lambda i, h: (i, 0, h)),
                  pl.BlockSpec((None, t, DK), lambda i, h: (i, 0, kq + h)),
                  pl.BlockSpec((None, t, DV), lambda i, h: (i, 0, kv + h)),
                  pl.BlockSpec((None, t, DV), lambda i, h: (i, 0, ko + h)),
                  pl.BlockSpec((None, tc, DK), lambda i, h: (i, 0, h)),
                  pl.BlockSpec((None, tc, DK), lambda i, h: (i, 0, kq + h)),
                  pl.BlockSpec((None, tc, DV), lambda i, h: (i, 0, kv + h)),
                  pl.BlockSpec((None, 4, None, 1, t), lambda i, h: (i, 0, h, 0, 0)),
                  pl.BlockSpec((None, 4, None, 1, tc), lambda i, h: (i, 0, h, 0, 0)),
                  pl.BlockSpec((4, None, 1, 1), lambda i, h: (0, h, 0, 0)),
                  pl.BlockSpec((3, DK), lambda i, h: (0, h)),
                  pl.BlockSpec((3, DK), lambda i, h: (0, kq + h)),
                  pl.BlockSpec((1, DK), lambda i, h: (0, h)),
                  pl.BlockSpec((1, DK), lambda i, h: (0, kq + h)),
                  pl.BlockSpec((1, DV), lambda i, h: (0, h))],
        out_specs=pl.BlockSpec((None, t, DV), lambda i, h: (i, 0, h)),
        out_shape=jax.ShapeDtypeStruct((b, t, D_V), BF16),
        scratch_shapes=[pltpu.VMEM((t, DK), BF16), pltpu.VMEM((t, DK), BF16),
                        pltpu.VMEM((t, DV), F32), pltpu.VMEM((t, DV), F32)],
        compiler_params=_cparams(("parallel", "parallel")),
        name="mlstm",
    )(px, px, px, px, pc, pc, pc, gx_rows, gc_rows, bif, w_qkc, w_qkc, b_qkc, b_qkc, mh_g)


CONV_PAD_LO = 16
CONV_SEG_ROWS = CONV_PAD_LO + GRID_W + 16


def _glu_conv_kernel(a_ref, gl_ref, w_ref, b_ref, o_ref, ypad_ref):
    tm, cb = a_ref.shape
    n_seg = tm // GRID_W
    zeros_pad = jnp.zeros((16, cb), F32)

    def fill(seg, carry):
        r0 = pl.multiple_of(seg * GRID_W, GRID_W)
        a = a_ref[pl.ds(r0, GRID_W), :].astype(F32)
        gl = gl_ref[pl.ds(r0, GRID_W), :].astype(F32)
        ypad_ref[seg, 0:CONV_PAD_LO, :] = zeros_pad
        ypad_ref[seg, CONV_PAD_LO:CONV_PAD_LO + GRID_W, :] = a * _sigmoid(gl)
        ypad_ref[seg, CONV_PAD_LO + GRID_W:CONV_SEG_ROWS, :] = zeros_pad
        return carry

    lax.fori_loop(0, n_seg, fill, 0)

    def conv(seg, carry):
        r0 = pl.multiple_of(seg * GRID_W, GRID_W)
        for lb in range(cb // LANES):
            ls = slice(lb * LANES, (lb + 1) * LANES)
            p = ypad_ref[seg, :, ls]
            acc = jnp.zeros((GRID_W, LANES), F32) + b_ref[:, ls]
            off0 = CONV_PAD_LO - CONV_HALF
            for r in range(SUBLANES):
                p_r = p if r == 0 else pltpu.roll(p, CONV_SEG_ROWS - r, 0)
                for a in range((off0 + CONV_W - 1) // SUBLANES + 1):
                    k = SUBLANES * a + r - off0
                    if 0 <= k < CONV_W:
                        acc = acc + p_r[SUBLANES * a:SUBLANES * a + GRID_W, :] * w_ref[k:k + 1, ls]
            o_ref[pl.ds(r0, GRID_W), ls] = acc.astype(BF16)
        return carry

    lax.fori_loop(0, n_seg, conv, 0)


def _glu_conv(px2d, w_dw, b_dw, tm=1024, cb=512):
    m = px2d.shape[0]
    a0 = (2 * D_QK + 2 * D_V) // cb
    g0 = a0 + D_MODEL // cb
    return pl.pallas_call(
        _glu_conv_kernel,
        grid=(m // tm, D_MODEL // cb),
        in_specs=[pl.BlockSpec((tm, cb), lambda i, j: (i, a0 + j)),
                  pl.BlockSpec((tm, cb), lambda i, j: (i, g0 + j)),
                  pl.BlockSpec((CONV_W, cb), lambda i, j: (0, j)),
                  pl.BlockSpec((1, cb), lambda i, j: (0, j))],
        out_specs=pl.BlockSpec((tm, cb), lambda i, j: (i, j)),
        out_shape=jax.ShapeDtypeStruct((m, D_MODEL), BF16),
        scratch_shapes=[pltpu.VMEM((tm // GRID_W, CONV_SEG_ROWS, cb), F32)],
        compiler_params=_cparams(("parallel", "parallel")),
        name="glu_dwconv",
    )(px2d, px2d, w_dw, b_dw)


def _merge_kernel(hm_ref, yc_ref, ga_ref, gb_ref, clg_ref, clb_ref, wpm_ref, wpc_ref, o_ref, ycn_ref):
    @pl.when(pl.program_id(1) == 0)
    def _():
        def ln_silu(rs):
            y = _ln_rows(yc_ref[rs, :].astype(F32)) * clg_ref[...] + clb_ref[...]
            ycn_ref[rs, :] = _silu(y).astype(BF16)

        _for_row_chunks(yc_ref.shape[0], ln_silu)

    y_m = jnp.dot(hm_ref[...], wpm_ref[...], preferred_element_type=F32)
    y_c = jnp.dot(ycn_ref[...], wpc_ref[...], preferred_element_type=F32)
    merged = _sigmoid(ga_ref[...].astype(F32)) * y_m + _sigmoid(gb_ref[...].astype(F32)) * y_c
    o_ref[...] = merged.astype(BF16)


def _merge(hm2d, yconv, px2d, cln_g, cln_b, w_pm, w_pc, tm=1024, tn=512):
    m = hm2d.shape[0]
    ga0 = (N_MAIN - 2 * D_MODEL) // tn
    gb0 = ga0 + D_MODEL // tn
    return pl.pallas_call(
        _merge_kernel,
        grid=(m // tm, D_MODEL // tn),
        in_specs=[pl.BlockSpec((tm, D_V), lambda i, j: (i, 0)),
                  pl.BlockSpec((tm, D_MODEL), lambda i, j: (i, 0)),
                  pl.BlockSpec((tm, tn), lambda i, j: (i, ga0 + j)),
                  pl.BlockSpec((tm, tn), lambda i, j: (i, gb0 + j)),
                  pl.BlockSpec((1, D_MODEL), lambda i, j: (0, 0)),
                  pl.BlockSpec((1, D_MODEL), lambda i, j: (0, 0)),
                  pl.BlockSpec((D_V, tn), lambda i, j: (0, j)),
                  pl.BlockSpec((D_MODEL, tn), lambda i, j: (0, j))],
        out_specs=pl.BlockSpec((tm, tn), lambda i, j: (i, j)),
        out_shape=jax.ShapeDtypeStruct((m, D_MODEL), BF16),
        scratch_shapes=[pltpu.VMEM((tm, D_MODEL), BF16)],
        compiler_params=_cparams(("parallel", "arbitrary")),
        name="branch_merge",
    )(hm2d, yconv, px2d, px2d, cln_g, cln_b, w_pm, w_pc)


def _outproj_kernel(mg_ref, x_ref, g1_ref, sh2_ref, sc2_ref, lg_ref, lb_ref, wo_ref, x1_ref, u2_ref, y_ref):
    y_ref[...] = jnp.dot(mg_ref[...], wo_ref[...], preferred_element_type=F32)

    def post_ln(rs):
        x1 = _ln_rows(ALPHA * x_ref[rs, :] + g1_ref[...] * y_ref[rs, :]) * lg_ref[...] + lb_ref[...]
        x1_ref[rs, :] = x1
        u2_ref[rs, :] = (_ln_rows(x1) * (1.0 + sc2_ref[...]) + sh2_ref[...]).astype(BF16)

    _for_row_chunks(x_ref.shape[0], post_ln)


def _out_projection(merged, x2d, g1, sh2, sc2, ln_g, ln_b, w_o, rows_per_batch, tm=256):
    m, d = x2d.shape
    tpb = rows_per_batch // tm
    per_batch = pl.BlockSpec((None, 1, d), lambda i: (i // tpb, 0, 0))
    vec = pl.BlockSpec((1, d), lambda i: (0, 0))
    tile = pl.BlockSpec((tm, d), lambda i: (i, 0))
    return pl.pallas_call(
        _outproj_kernel,
        grid=(m // tm,),
        in_specs=[tile, tile, per_batch, per_batch, per_batch, vec, vec,
                  pl.BlockSpec((d, d), lambda i: (0, 0))],
        out_specs=[tile, tile],
        out_shape=[jax.ShapeDtypeStruct((m, d), F32), jax.ShapeDtypeStruct((m, d), BF16)],
        scratch_shapes=[pltpu.VMEM((tm, d), F32)],
        compiler_params=_cparams(("parallel",)),
        name="out_projection",
    )(merged, x2d, g1, sh2, sc2, ln_g, ln_b, w_o)


def _ffn_kernel(u_ref, x1_ref, g2_ref, lg_ref, lb_ref, wa_ref, wb_ref, wo_ref, o_ref, acc_ref):
    f = pl.program_id(1)

    @pl.when(f == 0)
    def _():
        acc_ref[...] = jnp.zeros_like(acc_ref)

    u = u_ref[...]
    a = jnp.dot(u, wa_ref[...], preferred_element_type=F32)
    b = jnp.dot(u, wb_ref[...], preferred_element_type=F32)
    hid = (_silu(a) * b).astype(BF16)
    acc_ref[...] += jnp.dot(hid, wo_ref[...], preferred_element_type=F32)

    @pl.when(f == pl.num_programs(1) - 1)
    def _():
        def post_ln(rs):
            y = ALPHA * x1_ref[rs, :] + g2_ref[...] * acc_ref[rs, :]
            o_ref[rs, :] = _ln_rows(y) * lg_ref[...] + lb_ref[...]

        _for_row_chunks(x1_ref.shape[0], post_ln)


def _ffn(u2, x1, g2, ln_g, ln_b, w_ffn_in, w_ffn_out, rows_per_batch, tm=512, tf=512):
    m, d = x1.shape
    tpb = rows_per_batch // tm
    nf = D_FF // tf
    tile = pl.BlockSpec((tm, d), lambda i, f: (i, 0))
    vec = pl.BlockSpec((1, d), lambda i, f: (0, 0))
    return pl.pallas_call(
        _ffn_kernel,
        grid=(m // tm, nf),
        in_specs=[tile, tile,
                  pl.BlockSpec((None, 1, d), lambda i, f: (i // tpb, 0, 0)),
                  vec, vec,
                  pl.BlockSpec((d, tf), lambda i, f: (0, f)),
                  pl.BlockSpec((d, tf), lambda i, f: (0, nf + f)),
                  pl.BlockSpec((tf, d), lambda i, f: (f, 0))],
        out_specs=tile,
        out_shape=jax.ShapeDtypeStruct((m, d), F32),
        scratch_shapes=[pltpu.VMEM((tm, d), F32)],
        compiler_params=_cparams(("parallel", "arbitrary")),
        name="swiglu_ffn",
    )(u2, x1, g2, ln_g, ln_b, w_ffn_in, w_ffn_in, w_ffn_out)


def _gate_rows(g, batch, t):
    g = g[:, :N_GATES].reshape(batch, t, 4, N_HEADS)
    return jnp.transpose(g, (0, 2, 3, 1))[:, :, :, None, :]


def kernel(x, c, ctx, c_ctx, w_mod, b_mod, w_in, b_if, w_qk_conv, b_qk_conv, mh_norm_g, w_dw, b_dw,
           conv_ln_g, conv_ln_b, w_proj_m, w_proj_c, w_out, ln1_g, ln1_b, w_ffn_in, w_ffn_out, ln2_g, ln2_b):
    assert w_in.shape[0] == DEPTH == 1
    batch, t, d = x.shape
    tc = ctx.shape[1]
    row = lambda v: v[0].reshape(1, -1)

    w_in0 = w_in[0]
    w_a = w_in0[:, :GATE_COL0].astype(BF16)
    w_b = w_in0[:, GATE_COL0 + N_GATES:].astype(BF16)
    w_gate = jnp.pad(w_in0[:, GATE_COL0:GATE_COL0 + N_GATES], ((0, 0), (0, LANES - N_GATES))).astype(BF16)
    w_pm, w_pc, w_o = w_proj_m[0].astype(BF16), w_proj_c[0].astype(BF16), w_out[0].astype(BF16)
    w_f_in, w_f_out = w_ffn_in[0].astype(BF16), w_ffn_out[0].astype(BF16)

    c_rows = 8
    c_all = jnp.concatenate([c, c_ctx[None, :], jnp.zeros((c_rows - batch - 1, d), F32)], axis=0)
    mods = _modulation(c_all, w_mod[0], b_mod[0].reshape(1, -1))
    sh1, sc1, g1, sh2, sc2, g2 = [mods[:batch, k * d:(k + 1) * d].reshape(batch, 1, d) for k in range(6)]
    csh1 = mods[batch:batch + 1, 0:d].reshape(1, 1, d)
    csc1 = mods[batch:batch + 1, d:2 * d].reshape(1, 1, d)

    x2d = x.reshape(batch * t, d)
    px, gx = _in_projection(x2d, sh1, sc1, w_a, w_b, w_gate, N_MAIN, t, tm=1024, tn=1024)
    pc, gc = _in_projection(ctx.reshape(batch * tc, d), csh1, csc1, w_a, w_b, w_gate,
                            2 * D_QK + D_V, batch * tc, tm=batch * tc, tn=1024)

    bif = b_if[0].reshape(4, N_HEADS, 1, 1)
    hm = _mlstm(px.reshape(batch, t, N_MAIN), pc.reshape(batch, tc, -1),
                _gate_rows(gx, batch, t), _gate_rows(gc, batch, tc), bif,
                w_qk_conv[0], row(b_qk_conv), row(mh_norm_g))

    yconv = _glu_conv(px, w_dw[0], row(b_dw))

    merged = _merge(hm.reshape(batch * t, D_V), yconv, px, row(conv_ln_g), row(conv_ln_b), w_pm, w_pc)
    x1, u2 = _out_projection(merged, x2d, g1, sh2, sc2, row(ln1_g), row(ln1_b), w_o, t)
    out = _ffn(u2, x1, g2, row(ln2_g), row(ln2_b), w_f_in, w_f_out, t)
    return out.reshape(batch, t, d)
```

```python
import functools

import jax
import jax.numpy as jnp
from jax import lax
from jax.experimental import pallas as pl
from jax.experimental.pallas import tpu as pltpu

D_MODEL = 2048
GRID_W = 64
N_HEADS = 8
DK = 128
DV = 256
D_QK = N_HEADS * DK
D_V = N_HEADS * DV
CONV_W = 31
CONV_HALF = CONV_W // 2
N_GATES = 4 * N_HEADS
D_FF = 5632
DEPTH = 1
ALPHA = (2.0 * DEPTH) ** 0.25
LN_EPS = 1e-5
GATE_COL0 = 2 * D_QK + 2 * D_V

MLSTM_CHUNK = 256
NEG_BIG = -1e30
LANES = 128
SUBLANES = 8
ROW_CHUNK = 256
V7X_VMEM_LIMIT = 56 * 1024 * 1024

F32 = jnp.float32
BF16 = jnp.bfloat16


def _cparams(sem):
    return pltpu.CompilerParams(dimension_semantics=sem, vmem_limit_bytes=V7X_VMEM_LIMIT)


def _ln_rows(x):
    mu = jnp.mean(x, axis=-1, keepdims=True)
    xc = x - mu
    var = jnp.mean(xc * xc, axis=-1, keepdims=True)
    return xc * lax.rsqrt(var + LN_EPS)


def _sigmoid(x):
    return 1.0 / (1.0 + jnp.exp(-x))


def _silu(x):
    return x * _sigmoid(x)


def _log_sigmoid(x):
    return jnp.minimum(x, 0.0) - jnp.log1p(jnp.exp(-jnp.abs(x)))


def _for_row_chunks(n_rows, body):
    chunk = min(ROW_CHUNK, n_rows)

    def step(c, carry):
        body(pl.ds(pl.multiple_of(c * chunk, chunk), chunk))
        return carry

    lax.fori_loop(0, n_rows // chunk, step, 0)


def _mod_kernel(c_ref, w_ref, b_ref, o_ref):
    s = _silu(c_ref[...])
    o_ref[...] = jnp.dot(s, w_ref[...], preferred_element_type=F32,
                         precision=lax.Precision.HIGHEST) + b_ref[...]


def _modulation(c_all, w_mod, b_mod, tn=2048):
    rows, d = c_all.shape
    n = w_mod.shape[1]
    return pl.pallas_call(
        _mod_kernel,
        grid=(n // tn,),
        in_specs=[pl.BlockSpec((rows, d), lambda j: (0, 0)),
                  pl.BlockSpec((d, tn), lambda j: (0, j)),
                  pl.BlockSpec((1, tn), lambda j: (0, j))],
        out_specs=pl.BlockSpec((rows, tn), lambda j: (0, j)),
        out_shape=jax.ShapeDtypeStruct((rows, n), F32),
        compiler_params=_cparams(("arbitrary",)),
        name="adaln_mod",
    )(c_all, w_mod, b_mod)


def _dot_nt(a, b):
    return lax.dot_general(a, b, (((1,), (1,)), ((), ())), preferred_element_type=F32)


def _inproj_kernel(x_ref, sh_ref, sc_ref, wt_ref, wg_ref, o_ref, g_ref, u_ref):
    @pl.when(pl.program_id(1) == 0)
    def _():
        def ln_mod(rs):
            u = _ln_rows(x_ref[rs, :]) * (1.0 + sc_ref[...]) + sh_ref[...]
            ub = u.astype(BF16)
            u_ref[rs, :] = ub
            g_ref[rs, :] = _dot_nt(ub, wg_ref[...])

        _for_row_chunks(x_ref.shape[0], ln_mod)

    o_ref[...] = _dot_nt(u_ref[...], wt_ref[...].astype(BF16)).astype(BF16)


def _in_projection(x2d, shift, scale, w_t, w_gate_t, n_cols, rows_per_batch, tm, tn):
    m, d = x2d.shape
    tiles_per_batch = rows_per_batch // tm
    return pl.pallas_call(
        _inproj_kernel,
        grid=(m // tm, n_cols // tn),
        in_specs=[pl.BlockSpec((tm, d), lambda i, j: (i, 0)),
                  pl.BlockSpec((None, 1, d), lambda i, j: (i // tiles_per_batch, 0, 0)),
                  pl.BlockSpec((None, 1, d), lambda i, j: (i // tiles_per_batch, 0, 0)),
                  pl.BlockSpec((None, tn, d), lambda i, j: (0, j, 0)),
                  pl.BlockSpec((LANES, d), lambda i, j: (0, 0))],
        out_specs=[pl.BlockSpec((tm, tn), lambda i, j: (i, j)),
                   pl.BlockSpec((tm, LANES), lambda i, j: (i, 0)),
                   pl.BlockSpec((tm, d), lambda i, j: (i, 0))],
        out_shape=[jax.ShapeDtypeStruct((m, n_cols), BF16),
                   jax.ShapeDtypeStruct((m, LANES), F32),
                   jax.ShapeDtypeStruct((m, d), BF16)],
        compiler_params=_cparams(("parallel", "arbitrary")),
        name="in_projection",
    )(x2d, shift, scale, w_t, w_gate_t)


def _matmul_nt_kernel(u_ref, wt_ref, o_ref):
    o_ref[...] = _dot_nt(u_ref[...], wt_ref[0].astype(BF16)).astype(BF16)


def _conv_gate_projection(u, w_t, row0, n_cols, tm=1024, tn=1024):
    m, d = u.shape
    return pl.pallas_call(
        _matmul_nt_kernel,
        grid=(m // tm, n_cols // tn),
        in_specs=[pl.BlockSpec((tm, d), lambda i, j: (i, 0)),
                  pl.BlockSpec((pl.Element(1), pl.Element(tn), pl.Element(d)),
                               lambda i, j: (0, pl.multiple_of(row0 + j * tn, SUBLANES), 0))],
        out_specs=pl.BlockSpec((tm, tn), lambda i, j: (i, j)),
        out_shape=jax.ShapeDtypeStruct((m, n_cols), BF16),
        compiler_params=_cparams(("parallel", "arbitrary")),
        name="conv_gate_projection",
    )(u, w_t)


def _qk_conv_silu(x, w_ref, b_ref, scale):
    t = x.shape[0]
    row = lax.broadcasted_iota(jnp.int32, x.shape, 0)
    prev = jnp.where(row == 0, 0.0, pltpu.roll(x, 1, 0))
    nxt = jnp.where(row == t - 1, 0.0, pltpu.roll(x, t - 1, 0))
    y = prev * w_ref[0:1, :] + x * w_ref[1:2, :] + nxt * w_ref[2:3, :] + b_ref[...]
    y = _silu(y)
    if scale != 1.0:
        y = y * scale
    return y.astype(BF16)


def _mlstm_chunk(q, k, v, i_row, f_row, state, reverse, need_h):
    c_st, n_st, m_st = state
    length = q.shape[0]
    t_id = lax.broadcasted_iota(jnp.int32, (length, length), 0)
    s_id = lax.broadcasted_iota(jnp.int32, (length, length), 1)
    causal = (s_id >= t_id) if reverse else (s_id <= t_id)
    eye = s_id == t_id
    b_col = jnp.sum(jnp.where(causal, f_row, 0.0), axis=1, keepdims=True)
    b_row = jnp.sum(jnp.where(eye, b_col, 0.0), axis=0, keepdims=True)
    i_col = jnp.sum(jnp.where(eye, i_row, 0.0), axis=1, keepdims=True)
    h = None
    if need_h:
        d_log = jnp.where(causal, b_col + (i_row - b_row), NEG_BIG)
        inter = b_col + m_st
        m_t = jnp.maximum(inter, jnp.max(d_log, axis=1, keepdims=True))
        w_intra = jnp.exp(d_log - m_t)
        w_inter = jnp.exp(inter - m_t)
        s = lax.dot_general(q, k, (((1,), (1,)), ((), ())), preferred_element_type=F32) * w_intra
        q_c = jnp.dot(q, c_st.astype(BF16), preferred_element_type=F32)
        s_v = jnp.dot(s.astype(BF16), v, preferred_element_type=F32)
        num = w_inter * q_c + s_v
        q_n = jnp.sum(q.astype(F32) * n_st, axis=1, keepdims=True)
        den = w_inter * q_n + jnp.sum(s, axis=1, keepdims=True)
        h = num * (1.0 / jnp.maximum(jnp.abs(den), jnp.exp(-m_t)))
    b_end = b_col[0:1, :] if reverse else b_col[length - 1:length, :]
    w_log = b_end - b_col + i_col
    m_new = jnp.maximum(b_end + m_st, jnp.max(w_log, axis=0, keepdims=True))
    decay = jnp.exp(b_end + m_st - m_new)
    wk = jnp.exp(w_log - m_new) * k.astype(F32)
    c_new = decay * c_st + lax.dot_general(wk.astype(BF16), v, (((0,), (0,)), ((), ())),
                                           preferred_element_type=F32)
    n_new = decay * n_st + jnp.sum(wk, axis=0, keepdims=True)
    return h, (c_new, n_new, m_new)


def _mlstm_kernel(qx_ref, kx_ref, vx_ref, ox_ref, qc_ref, kc_ref, vc_ref, gx_ref, gc_ref, bif_ref,
                  wq_ref, wk_ref, bq_ref, bk_ref, mhg_ref, out_ref, qs_ref, ks_ref, hf_ref, hb_ref):
    t_len = qx_ref.shape[0]
    chunk = MLSTM_CHUNK
    n_chunks = t_len // chunk
    qs_ref[...] = _qk_conv_silu(qx_ref[...].astype(F32), wq_ref, bq_ref, DK ** -0.5)
    ks_ref[...] = _qk_conv_silu(kx_ref[...].astype(F32), wk_ref, bk_ref, 1.0)
    q_ctx = _qk_conv_silu(qc_ref[...].astype(F32), wq_ref, bq_ref, DK ** -0.5)
    k_ctx = _qk_conv_silu(kc_ref[...].astype(F32), wk_ref, bk_ref, 1.0)
    v_ctx = vc_ref[...]

    def gates(g_ref, idx):
        return g_ref[idx] + bif_ref[idx]

    zero_state = (jnp.zeros((DK, DV), F32), jnp.zeros((1, DK), F32), jnp.zeros((1, 1), F32))
    _, st_f = _mlstm_chunk(q_ctx, k_ctx, v_ctx, gates(gc_ref, 0), _log_sigmoid(gates(gc_ref, 1)),
                           zero_state, reverse=False, need_h=False)
    _, st_b = _mlstm_chunk(q_ctx, k_ctx, v_ctx, gates(gc_ref, 2), _log_sigmoid(gates(gc_ref, 3)),
                           zero_state, reverse=True, need_h=False)

    def rows(j):
        return slice(j * chunk, (j + 1) * chunk)

    for j in range(n_chunks):
        rf = rows(j)
        h, st_f = _mlstm_chunk(qs_ref[rf, :], ks_ref[rf, :], vx_ref[rf, :],
                               gx_ref[0, :, rf] + bif_ref[0],
                               _log_sigmoid(gx_ref[1, :, rf] + bif_ref[1]),
                               st_f, reverse=False, need_h=True)
        hf_ref[rf, :] = h
        rb = rows(n_chunks - 1 - j)
        h, st_b = _mlstm_chunk(qs_ref[rb, :], ks_ref[rb, :], vx_ref[rb, :],
                               gx_ref[2, :, rb] + bif_ref[2],
                               _log_sigmoid(gx_ref[3, :, rb] + bif_ref[3]),
                               st_b, reverse=True, need_h=True)
        hb_ref[rb, :] = h

    for j in range(n_chunks):
        r = rows(j)
        hm = _ln_rows(hf_ref[r, :] + hb_ref[r, :]) * mhg_ref[...]
        out_ref[r, :] = (hm * _sigmoid(ox_ref[r, :].astype(F32))).astype(BF16)


def _mlstm(px, pc, gx_rows, gc_rows, bif, w_qkc, b_qkc, mh_g):
    b, t, _ = px.shape
    tc = pc.shape[1]
    kq = D_QK // DK
    kv = 2 * D_QK // DV
    ko = (2 * D_QK + D_V) // DV
    return pl.pallas_call(
        _mlstm_kernel,
        grid=(b, N_HEADS),
        in_specs=[pl.BlockSpec((None, t, DK), lambda i, h: (i, 0, h)),
                  pl.BlockSpec((None, t, DK), lambda i, h: (i, 0, kq + h)),
                  pl.BlockSpec((None, t, DV), lambda i, h: (i, 0, kv + h)),
                  pl.BlockSpec((None, t, DV), lambda i, h: (i, 0, ko + h)),
                  pl.BlockSpec((None, tc, DK), lambda i, h: (i, 0, h)),
                  pl.BlockSpec((None, tc, DK), lambda i, h: (i, 0, kq + h)),
                  pl.BlockSpec((None, tc, DV), lambda i, h: (i, 0, kv + h)),
                  pl.BlockSpec((None, 4, None, 1, t), lambda i, h: (i, 0, h, 0, 0)),
                  pl.BlockSpec((None, 4, None, 1, tc), lambda i, h: (i, 0, h, 0, 0)),
                  pl.BlockSpec((4, None, 1, 1), lambda i, h: (0, h, 0, 0)),
                  pl.BlockSpec((3, DK), lambda i, h: (0, h)),
                  pl.BlockSpec((3, DK), lambda i, h: (0, kq + h)),
                  pl.BlockSpec((1, DK), lambda i, h: (0, h)),
                  pl.BlockSpec((1, DK), lambda i, h: (0, kq + h)),
                  pl.BlockSpec((1, DV), lambda i, h: (0, h))],
        out_specs=pl.BlockSpec((None, t, DV), lambda i, h: (i, 0, h)),
        out_shape=jax.ShapeDtypeStruct((b, t, D_V), BF16),
        scratch_shapes=[pltpu.VMEM((t, DK), BF16), pltpu.VMEM((t, DK), BF16),
                        pltpu.VMEM((t, DV), F32), pltpu.VMEM((t, DV), F32)],
        compiler_params=_cparams(("parallel", "parallel")),
        name="mlstm",
    )(px, px, px, px, pc, pc, pc, gx_rows, gc_rows, bif, w_qkc, w_qkc, b_qkc, b_qkc, mh_g)


CONV_PAD_LO = 16
CONV_SEG_ROWS = CONV_PAD_LO + GRID_W + 16


def _glu_conv_kernel(a_ref, gl_ref, w_ref, b_ref, o_ref, ypad_ref):
    tm, cb = a_ref.shape
    n_seg = tm // GRID_W
    zeros_pad = jnp.zeros((16, cb), F32)

    def fill(seg, carry):
        r0 = pl.multiple_of(seg * GRID_W, GRID_W)
        a = a_ref[pl.ds(r0, GRID_W), :].astype(F32)
        gl = gl_ref[pl.ds(r0, GRID_W), :].astype(F32)
        ypad_ref[seg, 0:CONV_PAD_LO, :] = zeros_pad
        ypad_ref[seg, CONV_PAD_LO:CONV_PAD_LO + GRID_W, :] = a * _sigmoid(gl)
        ypad_ref[seg, CONV_PAD_LO + GRID_W:CONV_SEG_ROWS, :] = zeros_pad
        return carry

    lax.fori_loop(0, n_seg, fill, 0)

    def conv(seg, carry):
        r0 = pl.multiple_of(seg * GRID_W, GRID_W)
        for lb in range(cb // LANES):
            ls = slice(lb * LANES, (lb + 1) * LANES)
            p = ypad_ref[seg, :, ls]
            acc = jnp.zeros((GRID_W, LANES), F32) + b_ref[:, ls]
            off0 = CONV_PAD_LO - CONV_HALF
            for r in range(SUBLANES):
                p_r = p if r == 0 else pltpu.roll(p, CONV_SEG_ROWS - r, 0)
                for a in range((off0 + CONV_W - 1) // SUBLANES + 1):
                    k = SUBLANES * a + r - off0
                    if 0 <= k < CONV_W:
                        acc = acc + p_r[SUBLANES * a:SUBLANES * a + GRID_W, :] * w_ref[k:k + 1, ls]
            o_ref[pl.ds(r0, GRID_W), ls] = acc.astype(BF16)
        return carry

    lax.fori_loop(0, n_seg, conv, 0)


def _glu_conv(px2d, w_dw, b_dw, tm=1024, cb=512):
    m = px2d.shape[0]
    a0 = 0
    g0 = D_MODEL // cb
    return pl.pallas_call(
        _glu_conv_kernel,
        grid=(m // tm, D_MODEL // cb),
        in_specs=[pl.BlockSpec((tm, cb), lambda i, j: (i, a0 + j)),
                  pl.BlockSpec((tm, cb), lambda i, j: (i, g0 + j)),
                  pl.BlockSpec((CONV_W, cb), lambda i, j: (0, j)),
                  pl.BlockSpec((1, cb), lambda i, j: (0, j))],
        out_specs=pl.BlockSpec((tm, cb), lambda i, j: (i, j)),
        out_shape=jax.ShapeDtypeStruct((m, D_MODEL), BF16),
        scratch_shapes=[pltpu.VMEM((tm // GRID_W, CONV_SEG_ROWS, cb), F32)],
        compiler_params=_cparams(("parallel", "parallel")),
        name="glu_dwconv",
    )(px2d, px2d, w_dw, b_dw)


def _merge_kernel(hm_ref, yc_ref, ga_ref, gb_ref, clg_ref, clb_ref, wpm_ref, wpc_ref, o_ref, ycn_ref):
    @pl.when(pl.program_id(1) == 0)
    def _():
        def ln_silu(rs):
            y = _ln_rows(yc_ref[rs, :].astype(F32)) * clg_ref[...] + clb_ref[...]
            ycn_ref[rs, :] = _silu(y).astype(BF16)

        _for_row_chunks(yc_ref.shape[0], ln_silu)

    y_m = jnp.dot(hm_ref[...], wpm_ref[...].astype(BF16), preferred_element_type=F32)
    y_c = jnp.dot(ycn_ref[...], wpc_ref[...].astype(BF16), preferred_element_type=F32)
    merged = _sigmoid(ga_ref[...].astype(F32)) * y_m + _sigmoid(gb_ref[...].astype(F32)) * y_c
    o_ref[...] = merged.astype(BF16)


def _merge(hm2d, yconv, px2d, cln_g, cln_b, w_pm, w_pc, tm=1024, tn=512):
    m = hm2d.shape[0]
    ga0 = 2 * D_MODEL // tn
    gb0 = ga0 + D_MODEL // tn
    return pl.pallas_call(
        _merge_kernel,
        grid=(m // tm, D_MODEL // tn),
        in_specs=[pl.BlockSpec((tm, D_V), lambda i, j: (i, 0)),
                  pl.BlockSpec((tm, D_MODEL), lambda i, j: (i, 0)),
                  pl.BlockSpec((tm, tn), lambda i, j: (i, ga0 + j)),
                  pl.BlockSpec((tm, tn), lambda i, j: (i, gb0 + j)),
                  pl.BlockSpec((1, D_MODEL), lambda i, j: (0, 0)),
                  pl.BlockSpec((1, D_MODEL), lambda i, j: (0, 0)),
                  pl.BlockSpec((D_V, tn), lambda i, j: (0, j)),
                  pl.BlockSpec((D_MODEL, tn), lambda i, j: (0, j))],
        out_specs=pl.BlockSpec((tm, tn), lambda i, j: (i, j)),
        out_shape=jax.ShapeDtypeStruct((m, D_MODEL), BF16),
        scratch_shapes=[pltpu.VMEM((tm, D_MODEL), BF16)],
        compiler_params=_cparams(("parallel", "arbitrary")),
        name="branch_merge",
    )(hm2d, yconv, px2d, px2d, cln_g, cln_b, w_pm, w_pc)


def _outproj_kernel(mg_ref, x_ref, g1_ref, sh2_ref, sc2_ref, lg_ref, lb_ref, wo_ref, x1_ref, u2_ref, y_ref):
    y_ref[...] = jnp.dot(mg_ref[...], wo_ref[...], preferred_element_type=F32)

    def post_ln(rs):
        x1 = _ln_rows(ALPHA * x_ref[rs, :] + g1_ref[...] * y_ref[rs, :]) * lg_ref[...] + lb_ref[...]
        x1_ref[rs, :] = x1
        u2_ref[rs, :] = (_ln_rows(x1) * (1.0 + sc2_ref[...]) + sh2_ref[...]).astype(BF16)

    _for_row_chunks(x_ref.shape[0], post_ln)


def _out_projection(merged, x2d, g1, sh2, sc2, ln_g, ln_b, w_o, rows_per_batch, tm=256):
    m, d = x2d.shape
    tpb = rows_per_batch // tm
    per_batch = pl.BlockSpec((None, 1, d), lambda i: (i // tpb, 0, 0))
    vec = pl.BlockSpec((1, d), lambda i: (0, 0))
    tile = pl.BlockSpec((tm, d), lambda i: (i, 0))
    return pl.pallas_call(
        _outproj_kernel,
        grid=(m // tm,),
        in_specs=[tile, tile, per_batch, per_batch, per_batch, vec, vec,
                  pl.BlockSpec((d, d), lambda i: (0, 0))],
        out_specs=[tile, tile],
        out_shape=[jax.ShapeDtypeStruct((m, d), F32), jax.ShapeDtypeStruct((m, d), BF16)],
        scratch_shapes=[pltpu.VMEM((tm, d), F32)],
        compiler_params=_cparams(("parallel",)),
        name="out_projection",
    )(merged, x2d, g1, sh2, sc2, ln_g, ln_b, w_o)


def _ffn_kernel(u_ref, x1_ref, g2_ref, lg_ref, lb_ref, wa_ref, wb_ref, wo_ref, o_ref, acc_ref):
    f = pl.program_id(1)

    @pl.when(f == 0)
    def _():
        acc_ref[...] = jnp.zeros_like(acc_ref)

    u = u_ref[...]
    a = jnp.dot(u, wa_ref[...], preferred_element_type=F32)
    b = jnp.dot(u, wb_ref[...], preferred_element_type=F32)
    hid = (_silu(a) * b).astype(BF16)
    acc_ref[...] += jnp.dot(hid, wo_ref[...], preferred_element_type=F32)

    @pl.when(f == pl.num_programs(1) - 1)
    def _():
        def post_ln(rs):
            y = ALPHA * x1_ref[rs, :] + g2_ref[...] * acc_ref[rs, :]
            o_ref[rs, :] = _ln_rows(y) * lg_ref[...] + lb_ref[...]

        _for_row_chunks(x1_ref.shape[0], post_ln)


def _ffn(u2, x1, g2, ln_g, ln_b, w_ffn_in, w_ffn_out, rows_per_batch, tm=512, tf=512):
    m, d = x1.shape
    tpb = rows_per_batch // tm
    nf = D_FF // tf
    tile = pl.BlockSpec((tm, d), lambda i, f: (i, 0))
    vec = pl.BlockSpec((1, d), lambda i, f: (0, 0))
    return pl.pallas_call(
        _ffn_kernel,
        grid=(m // tm, nf),
        in_specs=[tile, tile,
                  pl.BlockSpec((None, 1, d), lambda i, f: (i // tpb, 0, 0)),
                  vec, vec,
                  pl.BlockSpec((d, tf), lambda i, f: (0, f)),
                  pl.BlockSpec((d, tf), lambda i, f: (0, nf + f)),
                  pl.BlockSpec((tf, d), lambda i, f: (f, 0))],
        out_specs=tile,
        out_shape=jax.ShapeDtypeStruct((m, d), F32),
        scratch_shapes=[pltpu.VMEM((tm, d), F32)],
        compiler_params=_cparams(("parallel", "arbitrary")),
        name="swiglu_ffn",
    )(u2, x1, g2, ln_g, ln_b, w_ffn_in, w_ffn_in, w_ffn_out)


def _gate_rows(g, batch, t):
    g = g[:, :N_GATES].reshape(batch, t, 4, N_HEADS)
    return jnp.transpose(g, (0, 2, 3, 1))[:, :, :, None, :]


def kernel(x, c, ctx, c_ctx, w_mod, b_mod, w_in, b_if, w_qk_conv, b_qk_conv, mh_norm_g, w_dw, b_dw,
           conv_ln_g, conv_ln_b, w_proj_m, w_proj_c, w_out, ln1_g, ln1_b, w_ffn_in, w_ffn_out, ln2_g, ln2_b):
    assert w_in.shape[0] == DEPTH == 1
    batch, t, d = x.shape
    tc = ctx.shape[1]
    row = lambda v: v[0].reshape(1, -1)

    w_t = jnp.swapaxes(w_in, 1, 2)
    w_gate_t = jnp.pad(w_t[0, GATE_COL0:GATE_COL0 + N_GATES, :], ((0, LANES - N_GATES), (0, 0))).astype(BF16)
    w_o = w_out[0].astype(BF16)
    w_f_in, w_f_out = w_ffn_in[0].astype(BF16), w_ffn_out[0].astype(BF16)

    c_rows = 8
    c_all = jnp.concatenate([c, c_ctx[None, :], jnp.zeros((c_rows - batch - 1, d), F32)], axis=0)
    mods = _modulation(c_all, w_mod[0], b_mod[0].reshape(1, -1))
    sh1, sc1, g1, sh2, sc2, g2 = [mods[:batch, k * d:(k + 1) * d].reshape(batch, 1, d) for k in range(6)]
    csh1 = mods[batch:batch + 1, 0:d].reshape(1, 1, d)
    csc1 = mods[batch:batch + 1, d:2 * d].reshape(1, 1, d)

    x2d = x.reshape(batch * t, d)
    pq, gx, u = _in_projection(x2d, sh1, sc1, w_t, w_gate_t, GATE_COL0, t, tm=1024, tn=512)
    pc, gc, _ = _in_projection(ctx.reshape(batch * tc, d), csh1, csc1, w_t, w_gate_t,
                               2 * D_QK + D_V, batch * tc, tm=batch * tc, tn=512)
    pg = _conv_gate_projection(u, w_t, GATE_COL0 + N_GATES, 4 * D_MODEL)

    bif = b_if[0].reshape(4, N_HEADS, 1, 1)
    hm = _mlstm(pq.reshape(batch, t, GATE_COL0), pc.reshape(batch, tc, -1),
                _gate_rows(gx, batch, t), _gate_rows(gc, batch, tc), bif,
                w_qk_conv[0], row(b_qk_conv), row(mh_norm_g))

    yconv = _glu_conv(pg, w_dw[0], row(b_dw))

    merged = _merge(hm.reshape(batch * t, D_V), yconv, pg, row(conv_ln_g), row(conv_ln_b),
                    w_proj_m[0], w_proj_c[0])
    x1, u2 = _out_projection(merged, x2d, g1, sh2, sc2, row(ln1_g), row(ln1_b), w_o, t)
    out = _ffn(u2, x1, g2, row(ln2_g), row(ln2_b), w_f_in, w_f_out, t)
    return out.reshape(batch, t, d)
```
